```python
import jax, jax.numpy as jnp
from jax import lax
import numpy as np

D_MODEL = 2048
BATCH = 32
SEQ = 256
DEPTH = 4
DEC_BATCH = 8
DEC_SEQ = 1024
PAST_LEN = 256

GRID_W = 64
ROPE_BASE = 10000.0
EPS = 1e-6
Q_BLOCK = 128
MLA_HEADS = 8
MLA_NOPE = 64
MLA_ROPE = 32
MLA_V = 64
MLA_Q_RANK = 512
MLA_KV_RANK = 256
RET_HEADS = 4
RET_DK = 64
RET_DV = 128
RET_CHUNK = 128
CONV_CH = 512
CONV_W = 31
FNET_GROUPS = 4
FNET_GW = 128
N_BRANCH = 4
BRANCH_W = 512
N_EXPERTS = 32
TOP_K = 4
D_EXPERT = 2048
SWIGLU_ALPHA = 1.702
SWIGLU_LIMIT = 7.0
MOE_BLOCK = 128

IN_SIZES = (MLA_Q_RANK, MLA_KV_RANK, MLA_ROPE,
            RET_HEADS * RET_DK, RET_HEADS * RET_DK, RET_HEADS * RET_DV, RET_HEADS * RET_DV,
            2 * CONV_CH, FNET_GROUPS * FNET_GW, N_BRANCH * D_MODEL)
D_IN = sum(IN_SIZES)
IN_SPLITS = tuple(int(s) for s in np.cumsum(IN_SIZES)[:-1])

F32 = jnp.float32

kernel_name = 'hybrid_mla_retention_conv_fnet_moe_diffusion_step'


def rmsnorm(x, g):
    xf = x.astype(F32)
    return (xf * lax.rsqrt(jnp.mean(xf * xf, axis=-1, keepdims=True) + EPS)).astype(x.dtype) * g


def layernorm(x, g, b):
    xf = x.astype(F32)
    xc = xf - jnp.mean(xf, axis=-1, keepdims=True)
    return (xc * lax.rsqrt(jnp.mean(xc * xc, axis=-1, keepdims=True) + EPS)).astype(x.dtype) * g + b


def head_norm(y, g):
    yc = y - jnp.mean(y, axis=-1, keepdims=True)
    yn = yc * lax.rsqrt(jnp.mean(yc * yc, axis=-1, keepdims=True) + EPS)
    return yn.reshape(*y.shape[:-2], -1) * g.astype(F32)


def axial_rope(n_tokens, dim):
    rows = n_tokens // GRID_W
    row = jnp.repeat(jnp.arange(rows, dtype=F32), GRID_W)
    col = jnp.tile(jnp.arange(GRID_W, dtype=F32), rows)
    n_freq = dim // 4
    inv = ROPE_BASE ** (-jnp.arange(n_freq, dtype=F32) / n_freq)
    ang = jnp.concatenate([row[:, None] * inv, col[:, None] * inv], axis=-1)
    return jnp.cos(ang), jnp.sin(ang)


def apply_rope(x, cos, sin):
    xp = x.astype(F32).reshape(*x.shape[:-1], x.shape[-1] // 2, 2)
    c = cos[:, None, :]
    s = sin[:, None, :]
    x0, x1 = xp[..., 0], xp[..., 1]
    return jnp.stack([x0 * c - x1 * s, x0 * s + x1 * c], axis=-1).reshape(x.shape).astype(x.dtype)


def mla_attend(q_nope, q_rope, k_nope, k_rope, v):
    B, T, H, _ = q_nope.shape
    nb = T // Q_BLOCK
    scale = (MLA_NOPE + MLA_ROPE) ** -0.5

    def blocks(a):
        return jnp.moveaxis(a.reshape(B, nb, Q_BLOCK, *a.shape[2:]), 1, 0)

    def attend_block(qs):
        qn, qr = qs
        s = (jnp.einsum('bqhd,bkhd->bhqk', qn, k_nope, preferred_element_type=F32)
             + jnp.einsum('bqhd,bkd->bhqk', qr, k_rope, preferred_element_type=F32))
        p = jax.nn.softmax(s * scale, axis=-1).astype(v.dtype)
        return jnp.einsum('bhqk,bkhd->bqhd', p, v)

    o = lax.map(attend_block, (blocks(q_nope), blocks(q_rope)))
    return jnp.moveaxis(o, 0, 1).reshape(B, T, H * MLA_V)


def retention_chunkwise(q, k, v, log_gamma, s0):
    B, T, H, _ = q.shape
    dv = v.shape[-1]
    n = T // RET_CHUNK
    pos = jnp.arange(RET_CHUNK, dtype=F32)
    lg = log_gamma.astype(F32)[:, None, None]
    diff = pos[:, None] - pos[None, :]
    d_in = jnp.where(diff >= 0, jnp.exp(lg * jnp.maximum(diff, 0.0)), 0.0)
    q_dec = jnp.exp(lg[:, :, 0] * (pos + 1.0))[..., None]
    k_dec = jnp.exp(lg[:, :, 0] * (RET_CHUNK - 1.0 - pos))[..., None]
    c_dec = jnp.exp(lg * RET_CHUNK)

    def to_chunks(a):
        return a.astype(F32).reshape(B, n, RET_CHUNK, H, a.shape[-1]).transpose(1, 0, 3, 2, 4)

    def step(S, inp):
        qi, ki, vi = inp
        inner = jnp.einsum('bhid,bhjd->bhij', qi, ki) * d_in
        o = jnp.einsum('bhij,bhje->bhie', inner, vi) + jnp.einsum('bhid,bhde->bhie', qi * q_dec, S)
        S = c_dec * S + jnp.einsum('bhjd,bhje->bhde', ki * k_dec, vi)
        return S, o

    S, o = lax.scan(step, s0.astype(F32), (to_chunks(q), to_chunks(k), to_chunks(v)))
    return o.transpose(1, 0, 3, 2, 4).reshape(B, T, H, dv), S


def conformer_conv(u, w_dw, b_dw, g_ln, b_ln):
    a, b = jnp.split(u, 2, axis=-1)
    z = a * jax.nn.sigmoid(b)
    pad = CONV_W // 2
    z = lax.conv_general_dilated(z, w_dw[:, None, :].astype(z.dtype), window_strides=(1,),
                                 padding=[(pad, pad)], dimension_numbers=('NWC', 'WIO', 'NWC'),
                                 feature_group_count=CONV_CH) + b_dw
    return jax.nn.silu(layernorm(z, g_ln, b_ln))


def fourier_mix(u):
    B, T, _ = u.shape
    ug = u.astype(F32).reshape(B, T, FNET_GROUPS, FNET_GW)
    y = jnp.fft.fft2(ug, axes=(1, 3), norm='ortho').real
    return y.reshape(B, T, FNET_GROUPS * FNET_GW).astype(u.dtype)


def token_mixers(h, lp, ctx):
    B, T, _ = h.shape
    (q_lat, kv_lat, k_rope, rq, rk, rv, rg, conv_in, fnet_in, gate_in) = jnp.split(
        h @ lp['w_in'], IN_SPLITS, axis=-1)
    q = (rmsnorm(q_lat, lp['g_qn']) @ lp['w_q_up']).reshape(B, T, MLA_HEADS, MLA_NOPE + MLA_ROPE)
    q_nope, q_rope = q[..., :MLA_NOPE], q[..., MLA_NOPE:]
    c_kv = rmsnorm(kv_lat, lp['g_kvn'])
    rq = rq.reshape(B, T, RET_HEADS, RET_DK)
    rk = rk.reshape(B, T, RET_HEADS, RET_DK) * (RET_DK ** -0.5)
    rv = rv.reshape(B, T, RET_HEADS, RET_DV)
    if ctx is None:
        c_all, kr_all = c_kv, k_rope
        s0_f = jnp.zeros((B, RET_HEADS, RET_DK, RET_DV), F32)
        s0_b = s0_f
    else:
        cos_a, sin_a = axial_rope(T, MLA_ROPE)
        cos_r, sin_r = axial_rope(T, RET_DK)
        q_rope = apply_rope(q_rope, cos_a, sin_a)
        k_rope = apply_rope(k_rope[:, :, None, :], cos_a, sin_a)[:, :, 0, :]
        rq = apply_rope(rq, cos_r, sin_r)
        rk = apply_rope(rk, cos_r, sin_r)
        ckv_ctx, kr_ctx, s0_f, s0_b = ctx
        c_all = jnp.concatenate([ckv_ctx.astype(c_kv.dtype), c_kv], axis=1)
        kr_all = jnp.concatenate([kr_ctx.astype(k_rope.dtype), k_rope], axis=1)
    kv = (c_all @ lp['w_kv_up']).reshape(B, -1, MLA_HEADS, MLA_NOPE + MLA_V)
    y_mla = mla_attend(q_nope, q_rope, kv[..., :MLA_NOPE], kr_all, kv[..., MLA_NOPE:])
    log_gamma = jax.nn.log_sigmoid(lp['ret_decay'].astype(F32))
    o_f, s_f = retention_chunkwise(rq, rk, rv, log_gamma[0], s0_f)
    o_b, s_b = retention_chunkwise(rq[:, ::-1], rk[:, ::-1], rv[:, ::-1], log_gamma[1], s0_b)
    y_ret = (head_norm(o_f + o_b[:, ::-1], lp['g_ret']) * jax.nn.silu(rg.astype(F32))).astype(h.dtype)
    y_conv = conformer_conv(conv_in, lp['w_dw'], lp['b_dw'], lp['g_cln'], lp['b_cln'])
    y_fnet = fourier_mix(fnet_in)
    branches = jnp.stack([y_mla, y_ret, y_conv, y_fnet], axis=2)
    proj = jnp.einsum('btnc,ncd->btnd', branches, lp['w_branch'])
    gates = jax.nn.sigmoid(gate_in.reshape(B, T, N_BRANCH, D_MODEL))
    y = jnp.sum(gates * proj, axis=2) @ lp['w_out']
    if ctx is None:
        return y, (c_kv, k_rope, s_f, s_b)
    return y, None


def clamped_swiglu(g, u):
    g = jnp.minimum(g, SWIGLU_LIMIT)
    u = jnp.clip(u, -SWIGLU_LIMIT, SWIGLU_LIMIT)
    return g * jax.nn.sigmoid(SWIGLU_ALPHA * g) * (u + 1.0)


def moe_ffn(h, w_router, b_router, w_e_gate, b_e_gate, w_e_up, b_e_up, w_e_down, b_e_down):
    B, T, D = h.shape
    n_tok = B * T
    x = h.reshape(n_tok, D)
    logits = (x @ w_router + b_router).astype(F32)
    top_v, top_e = lax.top_k(logits, TOP_K)
    top_w = jax.nn.softmax(top_v, axis=-1)
    n_asg = n_tok * TOP_K
    flat_e = top_e.reshape(n_asg)
    order = jnp.argsort(flat_e)
    sorted_e = flat_e[order]
    counts = jnp.bincount(flat_e, length=N_EXPERTS)
    padded = (counts + MOE_BLOCK - 1) // MOE_BLOCK * MOE_BLOCK
    pad_end = jnp.cumsum(padded)
    pad_start = pad_end - padded
    start = jnp.cumsum(counts) - counts
    slot = pad_start[sorted_e] + jnp.arange(n_asg) - start[sorted_e]
    n_blocks = -(-n_asg // MOE_BLOCK) + N_EXPERTS
    n_slots = n_blocks * MOE_BLOCK
    slot_tok = jnp.full((n_slots,), n_tok, jnp.int32).at[slot].set((order // TOP_K).astype(jnp.int32))
    slot_w = jnp.zeros((n_slots,), F32).at[slot].set(top_w.reshape(n_asg)[order])
    block_e = jnp.minimum(jnp.searchsorted(pad_end, jnp.arange(n_blocks) * MOE_BLOCK, side='right'),
                          N_EXPERTS - 1)
    x_pad = jnp.concatenate([x, jnp.zeros((1, D), x.dtype)], axis=0)[slot_tok]
    x_pad = x_pad.reshape(n_blocks, MOE_BLOCK, D)

    def expert_block(args):
        xb, e = args
        g = xb @ w_e_gate[e] + b_e_gate[e]
        u = xb @ w_e_up[e] + b_e_up[e]
        return clamped_swiglu(g, u) @ w_e_down[e] + b_e_down[e]

    y_slots = lax.map(expert_block, (x_pad, block_e)).reshape(n_slots, D)
    y = jnp.zeros((n_tok + 1, D), F32).at[slot_tok].add(y_slots.astype(F32) * slot_w[:, None])
    return y[:n_tok].astype(h.dtype).reshape(B, T, D)


def trunk_layer(x, cvec, lp, ctx):
    m = (jax.nn.silu(cvec) @ lp['w_ada'] + lp['b_ada'])[:, None, :]
    sh1, sc1, gt1, sh2, sc2, gt2 = jnp.split(m, 6, axis=-1)
    h = rmsnorm(x, lp['g_norm1']) * (1.0 + sc1) + sh1
    mix, ctx_out = token_mixers(h, lp, ctx)
    x = x + gt1 * mix
    h = rmsnorm(x, lp['g_norm2']) * (1.0 + sc2) + sh2
    x = x + gt2 * moe_ffn(h, lp['w_router'], lp['b_router'], lp['w_e_gate'], lp['b_e_gate'],
                          lp['w_e_up'], lp['b_e_up'], lp['w_e_down'], lp['b_e_down'])
    return x, ctx_out


def setup_inputs(seed: int = 0) -> dict:
    key = jax.random.key(seed)
    keys = jax.random.split(key, 40)
    counter = [0]

    def nrm(shape, scale):
        k = keys[counter[0]]
        counter[0] += 1
        return scale * jax.random.normal(k, shape, jnp.float32)

    def gain(shape):
        return 1.0 + nrm(shape, 0.01)

    L, D = DEPTH, D_MODEL
    base_decay = jnp.log(2.0 ** (5.0 + jnp.arange(RET_HEADS, dtype=jnp.float32)) - 1.0)
    inp = {}
    inp['x_prompt'] = nrm((BATCH, SEQ, D), 1.0)
    inp['x_sample'] = nrm((DEC_BATCH, DEC_SEQ, D), 1.0)
    inp['c'] = nrm((DEC_BATCH, D), 1.0)
    inp['cache_ckv'] = nrm((DEC_BATCH, L, PAST_LEN, MLA_KV_RANK), 1.0)
    inp['cache_krope'] = nrm((DEC_BATCH, L, PAST_LEN, MLA_ROPE), 1.0)
    inp['state_ret'] = nrm((DEC_BATCH, L, 2, RET_HEADS, RET_DK, RET_DV), 1.0)
    inp['c_ctx'] = nrm((D,), 1.0)
    inp['w_ada'] = nrm((L, D, 6 * D), 0.5 * D ** -0.5)
    inp['b_ada'] = nrm((L, 6 * D), 0.02)
    inp['g_norm1'] = gain((L, D))
    inp['w_in'] = nrm((L, D, D_IN), D ** -0.5)
    inp['g_qn'] = gain((L, MLA_Q_RANK))
    inp['w_q_up'] = nrm((L, MLA_Q_RANK, MLA_HEADS * (MLA_NOPE + MLA_ROPE)), MLA_Q_RANK ** -0.5)
    inp['g_kvn'] = gain((L, MLA_KV_RANK))
    inp['w_kv_up'] = nrm((L, MLA_KV_RANK, MLA_HEADS * (MLA_NOPE + MLA_V)), MLA_KV_RANK ** -0.5)
    inp['ret_decay'] = base_decay + nrm((L, 2, RET_HEADS), 0.05)
    inp['g_ret'] = gain((L, RET_HEADS * RET_DV))
    inp['w_dw'] = nrm((L, CONV_W, CONV_CH), CONV_W ** -0.5)
    inp['b_dw'] = nrm((L, CONV_CH), 0.02)
    inp['g_cln'] = gain((L, CONV_CH))
    inp['b_cln'] = nrm((L, CONV_CH), 0.02)
    inp['w_branch'] = nrm((L, N_BRANCH, BRANCH_W, D), BRANCH_W ** -0.5)
    inp['w_out'] = nrm((L, D, D), D ** -0.5)
    inp['g_norm2'] = gain((L, D))
    inp['w_router'] = nrm((L, D, N_EXPERTS), D ** -0.5)
    inp['b_router'] = nrm((L, N_EXPERTS), 0.01)
    inp['w_e_gate'] = nrm((L, N_EXPERTS, D, D_EXPERT), D ** -0.5)
    inp['b_e_gate'] = nrm((L, N_EXPERTS, D_EXPERT), 0.01)
    inp['w_e_up'] = nrm((L, N_EXPERTS, D, D_EXPERT), D ** -0.5)
    inp['b_e_up'] = nrm((L, N_EXPERTS, D_EXPERT), 0.01)
    inp['w_e_down'] = nrm((L, N_EXPERTS, D_EXPERT, D), D_EXPERT ** -0.5)
    inp['b_e_down'] = nrm((L, N_EXPERTS, D), 0.01)
    inp['g_final'] = gain((D,))
    return inp


def reference(x_prompt, x_sample, c, cache_ckv, cache_krope, state_ret, c_ctx,
              w_ada, b_ada, g_norm1, w_in, g_qn, w_q_up, g_kvn, w_kv_up, ret_decay, g_ret,
              w_dw, b_dw, g_cln, b_cln, w_branch, w_out, g_norm2, w_router, b_router,
              w_e_gate, b_e_gate, w_e_up, b_e_up, w_e_down, b_e_down, g_final):
    xp = x_prompt
    xs = x_sample
    cvec_ctx = c_ctx[None, :]
    ckvs, krs, rets = [], [], []
    for l in range(DEPTH):
        lp = {'w_ada': w_ada[l], 'b_ada': b_ada[l], 'g_norm1': g_norm1[l], 'w_in': w_in[l],
              'g_qn': g_qn[l], 'w_q_up': w_q_up[l], 'g_kvn': g_kvn[l], 'w_kv_up': w_kv_up[l],
              'ret_decay': ret_decay[l], 'g_ret': g_ret[l], 'w_dw': w_dw[l], 'b_dw': b_dw[l],
              'g_cln': g_cln[l], 'b_cln': b_cln[l], 'w_branch': w_branch[l], 'w_out': w_out[l],
              'g_norm2': g_norm2[l], 'w_router': w_router[l], 'b_router': b_router[l],
              'w_e_gate': w_e_gate[l], 'b_e_gate': b_e_gate[l], 'w_e_up': w_e_up[l],
              'b_e_up': b_e_up[l], 'w_e_down': w_e_down[l], 'b_e_down': b_e_down[l]}
        xp, (ckv, kr, s_f, s_b) = trunk_layer(xp, cvec_ctx, lp, None)
        ckvs.append(ckv)
        krs.append(kr)
        rets.append(jnp.stack([s_f, s_b], axis=1))
        ctx = (cache_ckv[:, l], cache_krope[:, l], state_ret[:, l, 0], state_ret[:, l, 1])
        xs, _ = trunk_layer(xs, c, lp, ctx)
    y_prompt = rmsnorm(xp, g_final)
    y_sample = rmsnorm(xs, g_final)
    new_ckv = jnp.stack(ckvs, axis=1)
    new_krope = jnp.stack(krs, axis=1)
    new_ret = jnp.stack(rets, axis=1)
    return (y_prompt, y_sample, new_ckv, new_krope, new_ret)
```

```python
import functools
from typing import NamedTuple

import numpy as np
import jax
import jax.numpy as jnp
from jax import lax
from jax.experimental import pallas as pl
from jax.experimental.pallas import tpu as pltpu

F32 = jnp.float32
BF16 = jnp.bfloat16

EPS = 1e-6
ROPE_BASE = 10000.0
GRID_W = 64
LANES = 128
HALF = 64
MLA_HEADS = 8
MLA_NOPE = 64
MLA_ROPE = 32
MLA_Q_RANK = 512
MLA_KV_RANK = 256
RET_HEADS = 4
RET_DK = 64
RET_DV = 128
CONV_CH = 512
CONV_W = 31
CONV_PAD = 16
FNET_GROUPS = 4
FNET_GW = 128
N_BRANCH = 4
BRANCH_W = 512
N_EXPERTS = 32
TOP_K = 4
SWIGLU_ALPHA = 1.702
SWIGLU_LIMIT = 7.0
MOD_ROWS = 16
VMEM_LIMIT = 56 * 1024 * 1024

U_CONV = 0
U_RQ = 1024
U_RK = 1536
U_RV = 2048
U_RG = 2560
U_FNET = 3072
U_QLAT = 3584
U_KVLAT = 4096
U_KROPE = 4352
U_COLS = 4608


class _Cfg(NamedTuple):
    d_model: int = 2048
    batch: int = 32
    seq: int = 256
    depth: int = 4
    dec_batch: int = 8
    dec_seq: int = 1024
    past: int = 256
    d_expert: int = 2048
    tm: int = 1024
    tn: int = 512
    tr: int = 512
    te: int = 512
    tf: int = 256
    tg: int = 256
    tc: int = 128
    tq: int = 256

    @property
    def n_prompt(self):
        return self.batch * self.seq

    @property
    def n_tok(self):
        return self.batch * self.seq + self.dec_batch * self.dec_seq


def _cparams(*sem):
    return pltpu.CompilerParams(dimension_semantics=sem, vmem_limit_bytes=VMEM_LIMIT)


def _mod_row(cfg, row):
    return jnp.where(row < cfg.n_prompt, 0, 1 + (row - cfg.n_prompt) // cfg.dec_seq)


def _ada_kernel(c_ref, w_ref, b_ref, o_ref):
    c = c_ref[...]
    a = (c * jax.nn.sigmoid(c)).astype(BF16)
    o_ref[0] = jnp.dot(a, w_ref[0].astype(BF16), preferred_element_type=F32) + b_ref[0]


def _ada(cfg, cvec, w_ada, b_ada):
    L, D, C = w_ada.shape
    tn = min(512, C)
    return pl.pallas_call(
        _ada_kernel,
        out_shape=jax.ShapeDtypeStruct((L, MOD_ROWS, C), F32),
        grid=(L, C // tn),
        in_specs=[pl.BlockSpec((MOD_ROWS, D), lambda l, j: (0, 0)),
                  pl.BlockSpec((1, D, tn), lambda l, j: (l, 0, j)),
                  pl.BlockSpec((1, 1, tn), lambda l, j: (l, 0, j))],
        out_specs=pl.BlockSpec((1, MOD_ROWS, tn), lambda l, j: (l, 0, j)),
        compiler_params=_cparams("parallel", "parallel"),
        name="ada",
    )(cvec, w_ada, b_ada.reshape(L, 1, C))


def _rms_mod(x, g, sc, sh):
    ms = jnp.mean(x * x, axis=-1, keepdims=True)
    return (x * lax.rsqrt(ms + EPS)) * g * (1.0 + sc) + sh


def _inproj_kernel(x_ref, g_ref, sc_ref, sh_ref, w_ref, u_ref, h_ref):
    @pl.when(pl.program_id(1) == 0)
    def _():
        h_ref[...] = _rms_mod(x_ref[...], g_ref[...], sc_ref[0], sh_ref[0]).astype(BF16)

    u_ref[...] = jnp.dot(h_ref[...], w_ref[...], preferred_element_type=F32)


def _inproj(cfg, x, g, mod, w):
    N, D = x.shape
    C = w.shape[1]
    tm, tn = cfg.tm, cfg.tn
    nd = D // D
    del nd
    return pl.pallas_call(
        _inproj_kernel,
        out_shape=(jax.ShapeDtypeStruct((N, C), F32), jax.ShapeDtypeStruct((N, D), BF16)),
        grid=(N // tm, C // tn),
        in_specs=[pl.BlockSpec((tm, D), lambda i, j: (i, 0)),
                  pl.BlockSpec((1, D), lambda i, j: (0, 0)),
                  pl.BlockSpec((1, 1, D), lambda i, j: (_mod_row(cfg, i * tm), 0, 1)),
                  pl.BlockSpec((1, 1, D), lambda i, j: (_mod_row(cfg, i * tm), 0, 0)),
                  pl.BlockSpec((D, tn), lambda i, j: (0, j))],
        out_specs=(pl.BlockSpec((tm, tn), lambda i, j: (i, j)),
                   pl.BlockSpec((tm, D), lambda i, j: (i, 0))),
        compiler_params=_cparams("parallel", "arbitrary"),
        name="inproj",
    )(x, g, mod, mod, w)


def _rope(x, c, sa, sb):
    return x * c + pltpu.roll(x, LANES - 1, 1) * sa + pltpu.roll(x, 1, 1) * sb


def _prep_kernel(ql_ref, kvl_ref, kr_ref, gq_ref, gkv_ref, wq_ref, rc_ref, rsa_ref, rsb_ref,
                 q_ref, ckv_ref, kro_ref):
    ql = ql_ref[...]
    qn = (ql * lax.rsqrt(jnp.mean(ql * ql, axis=-1, keepdims=True) + EPS)) * gq_ref[...]
    q = jnp.dot(qn.astype(BF16), wq_ref[...], preferred_element_type=F32)
    c, sa, sb = rc_ref[...], rsa_ref[...], rsb_ref[...]
    for h in range(MLA_HEADS):
        sl = slice(h * LANES, (h + 1) * LANES)
        q_ref[:, sl] = _rope(q[:, sl], c, sa, sb).astype(BF16)
    kvl = kvl_ref[...]
    ckv_ref[...] = (kvl * lax.rsqrt(jnp.mean(kvl * kvl, axis=-1, keepdims=True) + EPS)) * gkv_ref[...]
    kro_ref[...] = _rope(kr_ref[...], c, sa, sb)


def _prep(cfg, u, gq, gkv, wq, rope_tabs):
    N = u.shape[0]
    tr = cfg.tr
    n_prompt_tiles = cfg.n_prompt // tr
    per_seq = cfg.dec_seq // tr

    def tab_map(i):
        return (jnp.where(i < n_prompt_tiles, 0, per_seq + (i - n_prompt_tiles) % per_seq), 0)

    tab_spec = pl.BlockSpec((tr, LANES), tab_map)
    HW = MLA_HEADS * LANES
    return pl.pallas_call(
        _prep_kernel,
        out_shape=(jax.ShapeDtypeStruct((N, HW), BF16),
                   jax.ShapeDtypeStruct((N, MLA_KV_RANK), F32),
                   jax.ShapeDtypeStruct((N, LANES), F32)),
        grid=(N // tr,),
        in_specs=[pl.BlockSpec((tr, MLA_Q_RANK), lambda i: (i, U_QLAT // MLA_Q_RANK)),
                  pl.BlockSpec((tr, MLA_KV_RANK), lambda i: (i, U_KVLAT // MLA_KV_RANK)),
                  pl.BlockSpec((tr, LANES), lambda i: (i, U_KROPE // LANES)),
                  pl.BlockSpec((1, MLA_Q_RANK), lambda i: (0, 0)),
                  pl.BlockSpec((1, MLA_KV_RANK), lambda i: (0, 0)),
                  pl.BlockSpec((MLA_Q_RANK, HW), lambda i: (0, 0)),
                  tab_spec, tab_spec, tab_spec],
        out_specs=(pl.BlockSpec((tr, HW), lambda i: (i, 0)),
                   pl.BlockSpec((tr, MLA_KV_RANK), lambda i: (i, 0)),
                   pl.BlockSpec((tr, LANES), lambda i: (i, 0))),
        compiler_params=_cparams("parallel"),
        name="mla_prep",
    )(u, u, u, gq, gkv, wq, *rope_tabs)


def _attn_kernel(*refs, has_ctx, T, P, tq, scale):
    if has_ctx:
        q_ref, ckv_ref, kr_ref, cckv_ref, ckr_ref, wk_ref, wv_ref, o_ref, k_s, v_s = refs
    else:
        q_ref, ckv_ref, kr_ref, wk_ref, wv_ref, o_ref, k_s, v_s = refs

    @pl.when(pl.program_id(1) == 0)
    def _():
        def fill(c, kr, off, n):
            cb = c.astype(BF16)
            kk = jnp.dot(cb, wk_ref[...], preferred_element_type=F32)
            vv = jnp.dot(cb, wv_ref[...], preferred_element_type=F32)
            lane = lax.broadcasted_iota(jnp.int32, (n, LANES), 1)
            for h in range(MLA_HEADS):
                k_s[h, off:off + n, :] = (kk[:, h * LANES:(h + 1) * LANES] + kr).astype(BF16)
                vp = vv[:, (h // 2) * LANES:(h // 2 + 1) * LANES]
                keep = (lane < HALF) if h % 2 == 0 else (lane >= HALF)
                v_s[h, off:off + n, :] = jnp.where(keep, vp, 0.0).astype(BF16)

        if has_ctx:
            fill(cckv_ref[0], ckr_ref[0], 0, P)
        fill(ckv_ref[...], kr_ref[...], P, T)

    q = q_ref[...]
    for p in range(MLA_HEADS // 2):
        acc = jnp.zeros((tq, LANES), F32)
        for h in (2 * p, 2 * p + 1):
            s = lax.dot_general(q[:, h * LANES:(h + 1) * LANES], k_s[h], (((1,), (1,)), ((), ())),
                                preferred_element_type=F32) * scale
            m = jnp.max(s, axis=-1, keepdims=True)
            e = jnp.exp(s - m)
            r = 1.0 / jnp.sum(e, axis=-1, keepdims=True)
            acc = acc + jnp.dot((e * r).astype(BF16), v_s[h], preferred_element_type=F32)
        o_ref[:, p * LANES:(p + 1) * LANES] = acc.astype(BF16)


def _attention(cfg, q, ckv, kr, wk, wv, *, B, T, row0, ctx=None):
    N = q.shape[0]
    tq = min(cfg.tq, T)
    has_ctx = ctx is not None
    P = ctx[0].shape[1] if has_ctx else 0
    S = P + T
    HW = MLA_HEADS * LANES
    b0 = row0 // T
    q0 = row0 // tq
    nq = T // tq
    in_specs = [pl.BlockSpec((tq, HW), lambda b, i: (q0 + b * nq + i, 0)),
                pl.BlockSpec((T, MLA_KV_RANK), lambda b, i: (b0 + b, 0)),
                pl.BlockSpec((T, LANES), lambda b, i: (b0 + b, 0))]
    args = [q, ckv, kr]
    if has_ctx:
        in_specs += [pl.BlockSpec((1, P, MLA_KV_RANK), lambda b, i: (b, 0, 0)),
                     pl.BlockSpec((1, P, LANES), lambda b, i: (b, 0, 0))]
        args += list(ctx)
    in_specs += [pl.BlockSpec((MLA_KV_RANK, HW), lambda b, i: (0, 0)),
                 pl.BlockSpec((MLA_KV_RANK, MLA_HEADS * HALF), lambda b, i: (0, 0))]
    args += [wk, wv]
    kern = functools.partial(_attn_kernel, has_ctx=has_ctx, T=T, P=P, tq=tq,
                             scale=float((MLA_NOPE + MLA_ROPE) ** -0.5))
    return pl.pallas_call(
        kern,
        out_shape=jax.ShapeDtypeStruct((B * T, BRANCH_W), BF16),
        grid=(B, nq),
        in_specs=in_specs,
        out_specs=pl.BlockSpec((tq, BRANCH_W), lambda b, i: (b * nq + i, 0)),
        scratch_shapes=[pltpu.VMEM((MLA_HEADS, S, LANES), BF16), pltpu.VMEM((MLA_HEADS, S, LANES), BF16)],
        compiler_params=_cparams("parallel", "arbitrary"),
        name="mla_attn_ctx" if has_ctx else "mla_attn",
    )(*args)


def _ret_kernel(*refs, has_ctx, T, tq):
    if has_ctx:
        lg_ref, rq_ref, rk_ref, rv_ref, rg_ref, g_ref, rc_ref, rsa_ref, rsb_ref, s0_ref, o_ref = refs
    else:
        lg_ref, rq_ref, rk_ref, rv_ref, rg_ref, g_ref, o_ref, st_ref = refs
    h = pl.program_id(1)
    lgf = lg_ref[0, h]
    lgb = lg_ref[1, h]
    q = rq_ref[...]
    k = rk_ref[...] * (RET_DK ** -0.5)
    if has_ctx:
        c, sa, sb = rc_ref[...], rsa_ref[...], rsb_ref[...]
        q = _rope(q, c, sa, sb)
        k = _rope(k, c, sa, sb)
        zpad = jnp.zeros((LANES - RET_DK, RET_DV), F32)
        s0f = jnp.concatenate([s0_ref[0, 0, 0], zpad], axis=0).astype(BF16)
        s0b = jnp.concatenate([s0_ref[0, 1, 0], zpad], axis=0).astype(BF16)
    kb = k.astype(BF16)
    vb = rv_ref[...].astype(BF16)
    g = g_ref[...]
    for blk in range(T // tq):
        rows = slice(blk * tq, (blk + 1) * tq)
        qblk = q[rows]
        s = lax.dot_general(qblk.astype(BF16), kb, (((1,), (1,)), ((), ())), preferred_element_type=F32)
        n_i = lax.broadcasted_iota(jnp.int32, (tq, T), 0) + blk * tq
        m_i = lax.broadcasted_iota(jnp.int32, (tq, T), 1)
        diff = (n_i - m_i).astype(F32)
        dec = (jnp.where(diff >= 0.0, jnp.exp(lgf * jnp.maximum(diff, 0.0)), 0.0)
               + jnp.where(diff <= 0.0, jnp.exp(lgb * jnp.maximum(-diff, 0.0)), 0.0))
        o = jnp.dot((s * dec).astype(BF16), vb, preferred_element_type=F32)
        if has_ctx:
            n_col = (lax.broadcasted_iota(jnp.int32, (tq, 1), 0) + blk * tq).astype(F32)
            qf = qblk * jnp.exp(lgf * (n_col + 1.0))
            qb = qblk * jnp.exp(lgb * (float(T) - n_col))
            o = o + jnp.dot(qf.astype(BF16), s0f, preferred_element_type=F32)
            o = o + jnp.dot(qb.astype(BF16), s0b, preferred_element_type=F32)
        yc = o - jnp.mean(o, axis=-1, keepdims=True)
        yn = yc * lax.rsqrt(jnp.mean(yc * yc, axis=-1, keepdims=True) + EPS)
        rg = rg_ref[rows, :]
        o_ref[rows, :] = (yn * g * (rg * jax.nn.sigmoid(rg))).astype(BF16)
    if not has_ctx:
        m_col = lax.broadcasted_iota(jnp.int32, (T, 1), 0).astype(F32)
        kf = (k * jnp.exp(lgf * (float(T - 1) - m_col))).T.astype(BF16)
        kbw = (k * jnp.exp(lgb * m_col)).T.astype(BF16)
        st_ref[0, 0, 0] = jnp.dot(kf, vb, preferred_element_type=F32)[:RET_DK]
        st_ref[0, 1, 0] = jnp.dot(kbw, vb, preferred_element_type=F32)[:RET_DK]


def _retention(cfg, u, lg, g_ret, *, B, T, row0, ctx=None):
    has_ctx = ctx is not None
    tq = min(cfg.tq, T)
    b0 = row0 // T
    H = RET_HEADS

    def col(off):
        return lambda b, h: (b0 + b, off // LANES + h)

    in_specs = [pl.BlockSpec(memory_space=pltpu.SMEM),
                pl.BlockSpec((T, LANES), col(U_RQ)),
                pl.BlockSpec((T, LANES), col(U_RK)),
                pl.BlockSpec((T, LANES), col(U_RV)),
                pl.BlockSpec((T, LANES), col(U_RG)),
                pl.BlockSpec((1, LANES), lambda b, h: (0, h))]
    args = [lg, u, u, u, u, g_ret]
    y_shape = jax.ShapeDtypeStruct((B * T, H * RET_DV), BF16)
    y_spec = pl.BlockSpec((T, LANES), lambda b, h: (b, h))
    if has_ctx:
        tabs, s0 = ctx
        tab_spec = pl.BlockSpec((T, LANES), lambda b, h: (0, 0))
        in_specs += [tab_spec, tab_spec, tab_spec,
                     pl.BlockSpec((1, 2, 1, RET_DK, RET_DV), lambda b, h: (b, 0, h, 0, 0))]
        args += [*tabs, s0]
        out_shape, out_specs = y_shape, y_spec
    else:
        out_shape = (y_shape, jax.ShapeDtypeStruct((B, 2, H, RET_DK, RET_DV), F32))
        out_specs = (y_spec, pl.BlockSpec((1, 2, 1, RET_DK, RET_DV), lambda b, h: (b, 0, h, 0, 0)))
    return pl.pallas_call(
        functools.partial(_ret_kernel, has_ctx=has_ctx, T=T, tq=tq),
        out_shape=out_shape,
        grid=(B, H),
        in_specs=in_specs,
        out_specs=out_specs,
        compiler_params=_cparams("parallel", "parallel"),
        name="retention_ctx" if has_ctx else "retention",
    )(*args)


def _conv_kernel(x_ref, w_ref, b_ref, g_ref, bl_ref, o_ref, zp, *, T, chunk):
    a = x_ref[:, :CONV_CH]
    b = x_ref[:, CONV_CH:]
    zp[0:CONV_PAD, :] = jnp.zeros((CONV_PAD, CONV_CH), F32)
    zp[CONV_PAD:CONV_PAD + T, :] = a * jax.nn.sigmoid(b)
    zp[CONV_PAD + T:2 * CONV_PAD + T, :] = jnp.zeros((CONV_PAD, CONV_CH), F32)
    w = w_ref[...]
    half = CONV_W // 2
    for c0 in range(0, T, chunk):
        acc = jnp.zeros((chunk, CONV_CH), F32) + b_ref[...]
        for k in range(CONV_W):
            start = c0 + k - half + CONV_PAD
            acc = acc + zp[start:start + chunk, :] * w[k:k + 1, :]
        xc = acc - jnp.mean(acc, axis=-1, keepdims=True)
        y = xc * lax.rsqrt(jnp.mean(xc * xc, axis=-1, keepdims=True) + EPS) * g_ref[...] + bl_ref[...]
        o_ref[c0:c0 + chunk, :] = (y * jax.nn.sigmoid(y)).astype(BF16)


def _conv(cfg, u, w_dw, b_dw, g_cln, b_cln, *, B, T, row0):
    b0 = row0 // T
    vec = pl.BlockSpec((1, CONV_CH), lambda b: (0, 0))
    return pl.pallas_call(
        functools.partial(_conv_kernel, T=T, chunk=min(128, T)),
        out_shape=jax.ShapeDtypeStruct((B * T, CONV_CH), BF16),
        grid=(B,),
        in_specs=[pl.BlockSpec((T, 2 * CONV_CH), lambda b: (b0 + b, U_CONV // (2 * CONV_CH))),
                  pl.BlockSpec((CONV_W + 1, CONV_CH), lambda b: (0, 0)), vec, vec, vec],
        out_specs=pl.BlockSpec((T, CONV_CH), lambda b: (b, 0)),
        scratch_shapes=[pltpu.VMEM((T + 2 * CONV_PAD, CONV_CH), F32)],
        compiler_params=_cparams("parallel"),
        name="conformer_conv",
    )(u, w_dw, b_dw, g_cln, b_cln)


def _fnet_kernel(x_ref, ct_ref, st_ref, cc_ref, sc_ref, o_ref, *, scale):
    xb = x_ref[...].astype(BF16)
    a = jnp.dot(ct_ref[...], xb, preferred_element_type=F32)
    b = jnp.dot(st_ref[...], xb, preferred_element_type=F32)
    for g in range(FNET_GROUPS):
        sl = slice(g * FNET_GW, (g + 1) * FNET_GW)
        y = (jnp.dot(a[:, sl].astype(BF16), cc_ref[...], preferred_element_type=F32)
             - jnp.dot(b[:, sl].astype(BF16), sc_ref[...], preferred_element_type=F32))
        o_ref[:, sl] = (y * scale).astype(BF16)


def _dft_tables(n):
    j = np.arange(n, dtype=np.int64)
    ang = 2.0 * np.pi * ((j[:, None] * j[None, :]) % n).astype(np.float64) / n
    return jnp.asarray(np.cos(ang), dtype=BF16), jnp.asarray(np.sin(ang), dtype=BF16)


def _fnet(cfg, u, *, B, T, row0):
    b0 = row0 // T
    W = FNET_GROUPS * FNET_GW
    ct, st = _dft_tables(T)
    cc, sc = _dft_tables(FNET_GW)
    full = lambda r, c: pl.BlockSpec((r, c), lambda b: (0, 0))
    return pl.pallas_call(
        functools.partial(_fnet_kernel, scale=float((T * FNET_GW) ** -0.5)),
        out_shape=jax.ShapeDtypeStruct((B * T, W), BF16),
        grid=(B,),
        in_specs=[pl.BlockSpec((T, W), lambda b: (b0 + b, U_FNET // W)),
                  full(T, T), full(T, T), full(FNET_GW, FNET_GW), full(FNET_GW, FNET_GW)],
        out_specs=pl.BlockSpec((T, W), lambda b: (b, 0)),
        compiler_params=_cparams("parallel"),
        name="fourier_mix",
    )(u, ct, st, cc, sc)


def _merge_kernel(h_ref, y0_ref, y1_ref, y2_ref, y3_ref, g0_ref, g1_ref, g2_ref, g3_ref, wb_ref, o_ref):
    h = h_ref[...]
    ys = (y0_ref, y1_ref, y2_ref, y3_ref)
    gs = (g0_ref, g1_ref, g2_ref, g3_ref)
    acc = None
    for n in range(N_BRANCH):
        gate = jax.nn.sigmoid(jnp.dot(h, gs[n][...], preferred_element_type=F32))
        proj = jnp.dot(ys[n][...], wb_ref[n], preferred_element_type=F32)
        acc = gate * proj if acc is None else acc + gate * proj
    o_ref[...] = acc.astype(BF16)


def _merge(cfg, h, branches, w_gate, w_branch):
    N, D = h.shape
    tm, tn = cfg.tm, min(cfg.tn, D)
    nj = D // tn
    br_spec = pl.BlockSpec((tm, BRANCH_W), lambda i, j: (i, 0))

    def gate_spec(n):
        return pl.BlockSpec((D, tn), lambda i, j: (0, n * nj + j))

    return pl.pallas_call(
        _merge_kernel,
        out_shape=jax.ShapeDtypeStruct((N, D), BF16),
        grid=(N // tm, nj),
        in_specs=[pl.BlockSpec((tm, D), lambda i, j: (i, 0)), br_spec, br_spec, br_spec, br_spec,
                  gate_spec(0), gate_spec(1), gate_spec(2), gate_spec(3),
                  pl.BlockSpec((N_BRANCH, BRANCH_W, tn), lambda i, j: (0, 0, j))],
        out_specs=pl.BlockSpec((tm, tn), lambda i, j: (i, j)),
        compiler_params=_cparams("parallel", "arbitrary"),
        name="gated_merge",
    )(h, *branches, w_gate, w_gate, w_gate, w_gate, w_branch)


def _outproj_kernel(y_ref, w_ref, x_ref, gt_ref, o_ref):
    o_ref[...] = x_ref[...] + gt_ref[0] * jnp.dot(y_ref[...], w_ref[...], preferred_element_type=F32)


def _outproj(cfg, y, w, x, mod):
    N, D = x.shape
    tm, tn = cfg.tm, min(cfg.tn, D)
    nj = D // tn
    return pl.pallas_call(
        _outproj_kernel,
        out_shape=jax.ShapeDtypeStruct((N, D), F32),
        grid=(N // tm, nj),
        in_specs=[pl.BlockSpec((tm, D), lambda i, j: (i, 0)),
                  pl.BlockSpec((D, tn), lambda i, j: (0, j)),
                  pl.BlockSpec((tm, tn), lambda i, j: (i, j)),
                  pl.BlockSpec((1, 1, tn), lambda i, j: (_mod_row(cfg, i * tm), 0, 2 * nj + j))],
        out_specs=pl.BlockSpec((tm, tn), lambda i, j: (i, j)),
        compiler_params=_cparams("parallel", "arbitrary"),
        name="out_proj",
    )(y, w, x, mod)


def _to_token_rows(ref, val, n):
    ds = val.shape[1] // LANES
    for j in range(ds):
        ref[pl.ds(j, n, stride=ds), :] = val[:, j * LANES:(j + 1) * LANES]


def _from_token_rows(ref, j, n, ds):
    return ref[pl.ds(j, n, stride=ds), :]


def _router_kernel(x_ref, g_ref, sc_ref, sh_ref, w_ref, b_ref, h_ref, e_ref, p_ref, r_ref, c_ref, run, *, tr):
    @pl.when(pl.program_id(0) == 0)
    def _():
        run[...] = jnp.zeros_like(run)

    h = _rms_mod(x_ref[...], g_ref[...], sc_ref[0], sh_ref[0])
    hb = h.astype(BF16)
    _to_token_rows(h_ref, h, tr)
    lane = lax.broadcasted_iota(jnp.int32, (tr, LANES), 1)
    lanef = lane.astype(F32)
    logits = jnp.dot(hb, w_ref[...], preferred_element_type=F32) + b_ref[...]
    l = jnp.where(lane < N_EXPERTS, logits, -jnp.inf)
    tops, idxs, hots = [], [], []
    for _ in range(TOP_K):
        m = jnp.max(l, axis=-1, keepdims=True)
        idx = jnp.min(jnp.where(l == m, lanef, float(LANES)), axis=-1, keepdims=True)
        hot = lanef == idx
        l = jnp.where(hot, -jnp.inf, l)
        tops.append(m)
        idxs.append(idx)
        hots.append(hot)
    es = [jnp.exp(t - tops[0]) for t in tops]
    inv = 1.0 / (es[0] + es[1] + es[2] + es[3])
    cnt = jnp.zeros((tr, LANES), F32)
    for hot in hots:
        cnt = cnt + jnp.where(hot, 1.0, 0.0)
    row = lax.broadcasted_iota(jnp.int32, (tr, tr), 0)
    colm = lax.broadcasted_iota(jnp.int32, (tr, tr), 1)
    tri = jnp.where(colm < row, 1.0, 0.0).astype(BF16)
    before = jnp.dot(tri, cnt.astype(BF16), preferred_element_type=F32) + run[...]
    e_out = jnp.zeros((tr, LANES), F32)
    p_out = jnp.zeros((tr, LANES), F32)
    r_out = jnp.zeros((tr, LANES), F32)
    for k in range(TOP_K):
        rank = jnp.sum(jnp.where(hots[k], before, 0.0), axis=-1, keepdims=True)
        sel = lane == k
        e_out = jnp.where(sel, idxs[k], e_out)
        p_out = jnp.where(sel, es[k] * inv, p_out)
        r_out = jnp.where(sel, rank, r_out)
    e_ref[...] = e_out.astype(jnp.int32)
    p_ref[...] = p_out
    r_ref[...] = r_out.astype(jnp.int32)
    run[...] = run[...] + jnp.sum(cnt, axis=0, keepdims=True)
    c_ref[...] = jnp.broadcast_to(run[...], c_ref.shape)


def _router(cfg, x, g, mod, w_r, b_r):
    N, D = x.shape
    tr = cfg.tr
    ds = D // LANES
    lane_out = pl.BlockSpec((tr, LANES), lambda i: (i, 0))
    return pl.pallas_call(
        functools.partial(_router_kernel, tr=tr),
        out_shape=(jax.ShapeDtypeStruct((N * ds, LANES), F32),
                   jax.ShapeDtypeStruct((N, LANES), jnp.int32),
                   jax.ShapeDtypeStruct((N, LANES), F32),
                   jax.ShapeDtypeStruct((N, LANES), jnp.int32),
                   jax.ShapeDtypeStruct((8, LANES), F32)),
        grid=(N // tr,),
        in_specs=[pl.BlockSpec((tr, D), lambda i: (i, 0)),
                  pl.BlockSpec((1, D), lambda i: (0, 0)),
                  pl.BlockSpec((1, 1, D), lambda i: (_mod_row(cfg, i * tr), 0, 4)),
                  pl.BlockSpec((1, 1, D), lambda i: (_mod_row(cfg, i * tr), 0, 3)),
                  pl.BlockSpec((D, LANES), lambda i: (0, 0)),
                  pl.BlockSpec((1, LANES), lambda i: (0, 0))],
        out_specs=(pl.BlockSpec((tr * ds, LANES), lambda i: (i, 0)), lane_out, lane_out, lane_out,
                   pl.BlockSpec((8, LANES), lambda i: (0, 0))),
        scratch_shapes=[pltpu.VMEM((1, LANES), F32)],
        compiler_params=_cparams("arbitrary"),
        name="router",
    )(x, g, mod, mod, w_r, b_r)


def _token_copy(src, dst, s, d, sem, ds):
    return pltpu.make_async_copy(src.at[pl.ds(pl.multiple_of(s * ds, ds), ds)],
                                 dst.at[pl.ds(pl.multiple_of(d * ds, ds), ds)], sem)


def _dispatch_kernel(slot_ref, h_ref, xs_in_ref, xs_ref, sem, *, tg, ds):
    del xs_in_ref
    base = pl.program_id(0) * tg

    def issue(t, carry):
        row = base + t
        for k in range(TOP_K):
            _token_copy(h_ref, xs_ref, row, slot_ref[row * TOP_K + k], sem, ds).start()
        return carry

    lax.fori_loop(0, tg, issue, 0)

    def drain(t, carry):
        for _ in range(TOP_K):
            _token_copy(h_ref, xs_ref, 0, 0, sem, ds).wait()
        return carry

    lax.fori_loop(0, tg, drain, 0)


def _dispatch(cfg, slot_flat, h2, n_slots):
    ds = cfg.d_model // LANES
    N = h2.shape[0] // ds
    tg = cfg.tg
    xs0 = jnp.zeros((n_slots * ds, LANES), h2.dtype)
    return pl.pallas_call(
        functools.partial(_dispatch_kernel, tg=tg, ds=ds),
        out_shape=jax.ShapeDtypeStruct((n_slots * ds, LANES), h2.dtype),
        grid_spec=pltpu.PrefetchScalarGridSpec(
            num_scalar_prefetch=1,
            grid=(N // tg,),
            in_specs=[pl.BlockSpec(memory_space=pl.ANY), pl.BlockSpec(memory_space=pl.ANY)],
            out_specs=pl.BlockSpec(memory_space=pl.ANY),
            scratch_shapes=[pltpu.SemaphoreType.DMA(())]),
        input_output_aliases={2: 0},
        compiler_params=pltpu.CompilerParams(dimension_semantics=("arbitrary",), has_side_effects=True),
        name="moe_dispatch",
    )(slot_flat, h2, xs0)


def _moe_kernel(be_ref, nu_ref, xs_ref, wg_ref, bg_ref, wu_ref, bu_ref, wd_ref, bd_ref, o_ref, xb_ref, acc_ref,
                *, te, nf):
    del be_ref
    i = pl.program_id(0)
    f = pl.program_id(1)
    ds = xb_ref.shape[1] // LANES

    @pl.when(i < nu_ref[0])
    def _():
        @pl.when(f == 0)
        def _():
            for j in range(ds):
                xb_ref[:, j * LANES:(j + 1) * LANES] = _from_token_rows(xs_ref, j, te, ds).astype(BF16)
            acc_ref[...] = jnp.broadcast_to(bd_ref[0, 0], acc_ref.shape)

        xb = xb_ref[...]
        g = jnp.dot(xb, wg_ref[0, 0].astype(BF16), preferred_element_type=F32) + bg_ref[0, 0]
        u = jnp.dot(xb, wu_ref[0, 0].astype(BF16), preferred_element_type=F32) + bu_ref[0, 0]
        g = jnp.minimum(g, SWIGLU_LIMIT)
        u = jnp.clip(u, -SWIGLU_LIMIT, SWIGLU_LIMIT)
        a = g * jax.nn.sigmoid(SWIGLU_ALPHA * g) * (u + 1.0)
        acc_ref[...] += jnp.dot(a.astype(BF16), wd_ref[0, 0].astype(BF16), preferred_element_type=F32)

        @pl.when(f == nf - 1)
        def _():
            _to_token_rows(o_ref, acc_ref[...], te)

    @pl.when(jnp.logical_and(i >= nu_ref[0], f == 0))
    def _():
        o_ref[...] = jnp.zeros_like(o_ref)


def _experts(cfg, layer, block_e, n_used, xs, w_g, b_g, w_u, b_u, w_d, b_d):
    D = cfg.d_model
    ds = D // LANES
    n_slots = xs.shape[0] // ds
    F = w_g.shape[-1]
    te, tf = cfg.te, min(cfg.tf, F)
    nf = F // tf
    n_blocks = n_slots // te

    def blk(i, nu):
        return jnp.minimum(i, nu[0] - 1)

    def fidx(i, f, nu):
        return jnp.where(i < nu[0], f, nf - 1)

    return pl.pallas_call(
        functools.partial(_moe_kernel, te=te, nf=nf),
        out_shape=jax.ShapeDtypeStruct((n_slots * ds, LANES), F32),
        grid_spec=pltpu.PrefetchScalarGridSpec(
            num_scalar_prefetch=2,
            grid=(n_blocks, nf),
            in_specs=[
                pl.BlockSpec((te * ds, LANES), lambda i, f, be, nu: (blk(i, nu), 0)),
                pl.BlockSpec((1, 1, D, tf), lambda i, f, be, nu: (layer, be[blk(i, nu)], 0, fidx(i, f, nu))),
                pl.BlockSpec((1, 1, 1, tf), lambda i, f, be, nu: (layer, be[blk(i, nu)], 0, fidx(i, f, nu))),
                pl.BlockSpec((1, 1, D, tf), lambda i, f, be, nu: (layer, be[blk(i, nu)], 0, fidx(i, f, nu))),
                pl.BlockSpec((1, 1, 1, tf), lambda i, f, be, nu: (layer, be[blk(i, nu)], 0, fidx(i, f, nu))),
                pl.BlockSpec((1, 1, tf, D), lambda i, f, be, nu: (layer, be[blk(i, nu)], fidx(i, f, nu), 0)),
                pl.BlockSpec((1, 1, 1, D), lambda i, f, be, nu: (layer, be[blk(i, nu)], 0, 0)),
            ],
            out_specs=pl.BlockSpec((te * ds, LANES), lambda i, f, be, nu: (i, 0)),
            scratch_shapes=[pltpu.VMEM((te, D), BF16), pltpu.VMEM((te, D), F32)]),
        compiler_params=_cparams("arbitrary", "arbitrary"),
        name="moe_experts",
    )(block_e, n_used, xs, w_g, b_g, w_u, b_u, w_d, b_d)


def _combine_kernel(slot_ref, ys_ref, p_ref, x_ref, gt_ref, o_ref, buf, sem, *, tc, ds):
    base = pl.program_id(0) * tc

    def gather(t, k, s):
        return _token_copy(ys_ref, buf.at[k], s, t, sem, ds)

    def issue(t, carry):
        for k in range(TOP_K):
            gather(t, k, slot_ref[(base + t) * TOP_K + k]).start()
        return carry

    lax.fori_loop(0, tc, issue, 0)

    def drain(t, carry):
        for k in range(TOP_K):
            gather(0, k, 0).wait()
        return carry

    lax.fori_loop(0, tc, drain, 0)
    p = p_ref[...]
    for j in range(ds):
        sl = slice(j * LANES, (j + 1) * LANES)
        y = p[:, 0:1] * _from_token_rows(buf.at[0], j, tc, ds)
        for k in range(1, TOP_K):
            y = y + p[:, k:k + 1] * _from_token_rows(buf.at[k], j, tc, ds)
        o_ref[:, sl] = x_ref[:, sl] + gt_ref[0][:, sl] * y


def _combine(cfg, slot_flat, ys, top_p, x, mod):
    N, D = x.shape
    tc = cfg.tc
    ds = D // LANES
    return pl.pallas_call(
        functools.partial(_combine_kernel, tc=tc, ds=ds),
        out_shape=jax.ShapeDtypeStruct((N, D), F32),
        grid_spec=pltpu.PrefetchScalarGridSpec(
            num_scalar_prefetch=1,
            grid=(N // tc,),
            in_specs=[pl.BlockSpec(memory_space=pl.ANY),
                      pl.BlockSpec((tc, LANES), lambda i, s: (i, 0)),
                      pl.BlockSpec((tc, D), lambda i, s: (i, 0)),
                      pl.BlockSpec((1, 1, D), lambda i, s: (_mod_row(cfg, i * tc), 0, 5))],
            out_specs=pl.BlockSpec((tc, D), lambda i, s: (i, 0)),
            scratch_shapes=[pltpu.VMEM((TOP_K, tc * ds, LANES), F32), pltpu.SemaphoreType.DMA(())]),
        compiler_params=_cparams("arbitrary"),
        name="moe_combine",
    )(slot_flat, ys, top_p, x, mod)


def _final_kernel(x_ref, g_ref, o_ref):
    x = x_ref[...]
    o_ref[...] = (x * lax.rsqrt(jnp.mean(x * x, axis=-1, keepdims=True) + EPS)) * g_ref[...]


def _final_norm(cfg, x, g):
    N, D = x.shape
    tr = cfg.tr
    return pl.pallas_call(
        _final_kernel,
        out_shape=jax.ShapeDtypeStruct((N, D), F32),
        grid=(N // tr,),
        in_specs=[pl.BlockSpec((tr, D), lambda i: (i, 0)), pl.BlockSpec((1, D), lambda i: (0, 0))],
        out_specs=pl.BlockSpec((tr, D), lambda i: (i, 0)),
        compiler_params=_cparams("parallel"),
        name="final_norm",
    )(x, g)


def _axial_angles(n_tokens, dim):
    rows = n_tokens // GRID_W
    row = jnp.repeat(jnp.arange(rows, dtype=F32), GRID_W)
    col = jnp.tile(jnp.arange(GRID_W, dtype=F32), rows)
    n_freq = dim // 4
    inv = ROPE_BASE ** (-jnp.arange(n_freq, dtype=F32) / n_freq)
    ang = jnp.concatenate([row[:, None] * inv, col[:, None] * inv], axis=-1)
    return jnp.cos(ang), jnp.sin(ang)


def _rope_tables(n_tokens, dim, lane0, identity_rows):
    cos, sin = _axial_angles(n_tokens, dim)
    cos2 = jnp.repeat(cos, 2, axis=-1)
    sin2 = jnp.repeat(sin, 2, axis=-1)
    even = (jnp.arange(dim) % 2 == 0)[None, :]
    pad = ((identity_rows, 0), (lane0, LANES - lane0 - dim))
    c = jnp.pad(cos2, pad, constant_values=1.0)
    sa = jnp.pad(jnp.where(even, -sin2, 0.0), pad)
    sb = jnp.pad(jnp.where(even, 0.0, sin2), pad)
    return c, sa, sb


def _pad_heads(w, heads, width, to):
    lead = w.shape[:-1]
    w = w.reshape(*lead, heads, width)
    w = jnp.pad(w, [(0, 0)] * len(lead) + [(0, 0), (0, to - width)])
    return w.reshape(*lead, heads * to)


def _pack_w_in(w_in):
    L, D, _ = w_in.shape
    o = 0
    segs = {}
    for name, width in (("q", MLA_Q_RANK), ("kv", MLA_KV_RANK), ("kr", MLA_ROPE),
                        ("rq", RET_HEADS * RET_DK), ("rk", RET_HEADS * RET_DK),
                        ("rv", RET_HEADS * RET_DV), ("rg", RET_HEADS * RET_DV),
                        ("conv", 2 * CONV_CH), ("fnet", FNET_GROUPS * FNET_GW)):
        segs[name] = w_in[:, :, o:o + width]
        o += width
    kr = jnp.pad(segs["kr"], ((0, 0), (0, 0), (MLA_NOPE, LANES - MLA_NOPE - MLA_ROPE)))
    packed = jnp.concatenate(
        [segs["conv"], _pad_heads(segs["rq"], RET_HEADS, RET_DK, LANES),
         _pad_heads(segs["rk"], RET_HEADS, RET_DK, LANES), segs["rv"], segs["rg"], segs["fnet"],
         segs["q"], segs["kv"], kr, jnp.zeros((L, D, U_COLS - U_KROPE - LANES), w_in.dtype)], axis=-1)
    return packed.astype(BF16), w_in[:, :, o:].astype(BF16)


def _forward(cfg, x_prompt, x_sample, c, cache_ckv, cache_krope, state_ret, c_ctx,
             w_ada, b_ada, g_norm1, w_in, g_qn, w_q_up, g_kvn, w_kv_up, ret_decay, g_ret,
             w_dw, b_dw, g_cln, b_cln, w_branch, w_out, g_norm2, w_router, b_router,
             w_e_gate, b_e_gate, w_e_up, b_e_up, w_e_down, b_e_down, g_final):
    D, L = cfg.d_model, cfg.depth
    NP, N = cfg.n_prompt, cfg.n_tok
    B, T = cfg.batch, cfg.seq
    BS, TS = cfg.dec_batch, cfg.dec_seq
    F = cfg.d_expert

    x = jnp.concatenate([x_prompt.reshape(NP, D), x_sample.reshape(BS * TS, D)], axis=0)
    cvec = jnp.concatenate([c_ctx[None, :], c, jnp.zeros((MOD_ROWS - 1 - BS, D), F32)], axis=0)
    mods = _ada(cfg, cvec, w_ada, b_ada).reshape(L, MOD_ROWS, 1, 6 * D)

    w_small, w_gate = _pack_w_in(w_in)
    w_q = _pad_heads(w_q_up, MLA_HEADS, MLA_NOPE + MLA_ROPE, LANES).astype(BF16)
    kv4 = w_kv_up.reshape(L, MLA_KV_RANK, MLA_HEADS, 2 * HALF)
    w_k = _pad_heads(kv4[..., :HALF].reshape(L, MLA_KV_RANK, MLA_HEADS * HALF), MLA_HEADS, HALF, LANES).astype(BF16)
    w_v = kv4[..., HALF:].reshape(L, MLA_KV_RANK, MLA_HEADS * HALF).astype(BF16)
    w_br = w_branch.astype(BF16)
    w_o = w_out.astype(BF16)
    w_r = jnp.pad(w_router, ((0, 0), (0, 0), (0, LANES - N_EXPERTS))).astype(BF16)
    b_r = jnp.pad(b_router, ((0, 0), (0, LANES - N_EXPERTS)))
    w_conv = jnp.pad(w_dw, ((0, 0), (0, 1), (0, 0)))
    lg = jax.nn.log_sigmoid(ret_decay.astype(F32))
    cache_kr = jnp.pad(cache_krope, ((0, 0), (0, 0), (0, 0), (MLA_NOPE, LANES - MLA_NOPE - MLA_ROPE)))
    b_g4 = b_e_gate.reshape(L, N_EXPERTS, 1, F)
    b_u4 = b_e_up.reshape(L, N_EXPERTS, 1, F)
    b_d4 = b_e_down.reshape(L, N_EXPERTS, 1, D)

    mla_tabs = _rope_tables(TS, MLA_ROPE, MLA_NOPE, TS)
    ret_tabs = _rope_tables(TS, RET_DK, 0, 0)

    te = cfg.te
    n_blocks = N * TOP_K // te + N_EXPERTS
    n_slots = n_blocks * te

    ckvs, krs, rets = [], [], []
    for l in range(L):
        mod = mods[l]
        u, h = _inproj(cfg, x, g_norm1[l][None, :], mod, w_small[l])
        q, ckv, kr = _prep(cfg, u, g_qn[l][None, :], g_kvn[l][None, :], w_q[l], mla_tabs)
        ckvs.append(ckv[:NP].reshape(B, T, MLA_KV_RANK))
        krs.append(kr[:NP, MLA_NOPE:MLA_NOPE + MLA_ROPE].reshape(B, T, MLA_ROPE))

        ya_p = _attention(cfg, q, ckv, kr, w_k[l], w_v[l], B=B, T=T, row0=0)
        ya_s = _attention(cfg, q, ckv, kr, w_k[l], w_v[l], B=BS, T=TS, row0=NP,
                          ctx=(cache_ckv[:, l], cache_kr[:, l]))
        g_r = g_ret[l][None, :]
        yr_p, st = _retention(cfg, u, lg[l], g_r, B=B, T=T, row0=0)
        yr_s = _retention(cfg, u, lg[l], g_r, B=BS, T=TS, row0=NP, ctx=(ret_tabs, state_ret[:, l]))
        rets.append(st)
        conv_args = (w_conv[l], b_dw[l][None, :], g_cln[l][None, :], b_cln[l][None, :])
        yc_p = _conv(cfg, u, *conv_args, B=B, T=T, row0=0)
        yc_s = _conv(cfg, u, *conv_args, B=BS, T=TS, row0=NP)
        yf_p = _fnet(cfg, u, B=B, T=T, row0=0)
        yf_s = _fnet(cfg, u, B=BS, T=TS, row0=NP)
        branches = [jnp.concatenate([a, b], axis=0)
                    for a, b in ((ya_p, ya_s), (yr_p, yr_s), (yc_p, yc_s), (yf_p, yf_s))]

        merged = _merge(cfg, h, branches, w_gate[l], w_br[l])
        x = _outproj(cfg, merged, w_o[l], x, mod)

        h2, top_e, top_p, rank, counts = _router(cfg, x, g_norm2[l][None, :], mod, w_r[l], b_r[l][None, :])
        cnt = counts[0, :N_EXPERTS].astype(jnp.int32)
        padded = (cnt + te - 1) // te * te
        pad_end = jnp.cumsum(padded)
        pad_start = pad_end - padded
        slot = (pad_start[top_e[:, :TOP_K]] + rank[:, :TOP_K]).reshape(N * TOP_K)
        block_e = jnp.minimum(jnp.searchsorted(pad_end, jnp.arange(n_blocks, dtype=jnp.int32) * te, side='right'),
                              N_EXPERTS - 1).astype(jnp.int32)
        n_used = (pad_end[-1:] // te).astype(jnp.int32)

        xs = _dispatch(cfg, slot, h2, n_slots)
        ys = _experts(cfg, l, block_e, n_used, xs, w_e_gate, b_g4, w_e_up, b_u4, w_e_down, b_d4)
        x = _combine(cfg, slot, ys, top_p, x, mod)

    y = _final_norm(cfg, x, g_final[None, :])
    y_prompt = y[:NP].reshape(B, T, D)
    y_sample = y[NP:].reshape(BS, TS, D)
    return (y_prompt, y_sample, jnp.stack(ckvs, axis=1), jnp.stack(krs, axis=1), jnp.stack(rets, axis=1))


def kernel(x_prompt, x_sample, c, cache_ckv, cache_krope, state_ret, c_ctx, w_ada, b_ada, g_norm1, w_in, g_qn, w_q_up, g_kvn, w_kv_up, ret_decay, g_ret, w_dw, b_dw, g_cln, b_cln, w_branch, w_out, g_norm2, w_router, b_router, w_e_gate, b_e_gate, w_e_up, b_e_up, w_e_down, b_e_down, g_final):
    return _forward(_Cfg(), x_prompt, x_sample, c, cache_ckv, cache_krope, state_ret, c_ctx, w_ada, b_ada,
                    g_norm1, w_in, g_qn, w_q_up, g_kvn, w_kv_up, ret_decay, g_ret, w_dw, b_dw, g_cln, b_cln,
                    w_branch, w_out, g_norm2, w_router, b_router, w_e_gate, b_e_gate, w_e_up, b_e_up,
                    w_e_down, b_e_down, g_final)
```
